```python
import math
import numpy as np
import jax
import jax.numpy as jnp
from jax import lax

D_MODEL = 1024
BATCH = 4
SEQ = 4096
DEPTH = 1

SSD_WIDTH = D_MODEL
ATTN_WIDTH = D_MODEL
MIX_WIDTH = SSD_WIDTH + ATTN_WIDTH
SSD_HEAD_DIM = 64
SSD_HEADS = SSD_WIDTH // SSD_HEAD_DIM
SSD_GROUPS = 2
SSD_HEADS_PER_GROUP = SSD_HEADS // SSD_GROUPS
SSD_STATE = 128
SSD_CONV = 4
SSD_CHUNK = 128
SSD_CONV_DIM = SSD_WIDTH + 2 * SSD_GROUPS * SSD_STATE
ATTN_HEAD_DIM = 64
ATTN_HEADS = ATTN_WIDTH // ATTN_HEAD_DIM
KV_HEADS = 4
Q_PER_KV = ATTN_HEADS // KV_HEADS
KV_WIDTH = KV_HEADS * ATTN_HEAD_DIM
CMP_BLOCK = 32
CMP_STRIDE = 16
CMP_HIDDEN = 256
SLC_BLOCK = 64
SLC_TOPK = 16
WINDOW = 512
Q_BLOCK = 64
ROPE_THETA = 10000.0
N_GATES = 3
PEER_HEADS = 8
PEER_KEYS = 128
PEER_EXPERTS = PEER_KEYS * PEER_KEYS
PEER_KEY_DIM = 256
PEER_TOPK = 16
PEER_TOKEN_BLOCK = 128
DN_ALPHA = (2 * DEPTH) ** 0.25
DN_BETA = (8 * DEPTH) ** -0.25
LN_EPS = 1e-5
RMS_EPS = 1e-5
NEG_INF = -1e30
BIG = 1e30

SPLIT_SIZES = (SSD_WIDTH, SSD_CONV_DIM, SSD_HEADS, ATTN_WIDTH, 6 * KV_WIDTH, ATTN_HEADS * N_GATES)
IN_PROJ_DIM = sum(SPLIT_SIZES)
SPLIT_POINTS = tuple(int(v) for v in np.cumsum(SPLIT_SIZES)[:-1])

kernel_name = "hymba_ssd_nsa_peer_deepnorm"


def layer_norm(x, g, b):
    xf = x.astype(jnp.float32)
    mu = jnp.mean(xf, -1, keepdims=True)
    var = jnp.mean(jnp.square(xf - mu), -1, keepdims=True)
    return ((xf - mu) * lax.rsqrt(var + LN_EPS) * g + b).astype(x.dtype)


def rms_norm(x, g):
    xf = x.astype(jnp.float32)
    return xf * lax.rsqrt(jnp.mean(xf * xf, -1, keepdims=True) + RMS_EPS) * g


def rope_angles(pos):
    inv = ROPE_THETA ** (-jnp.arange(0, ATTN_HEAD_DIM, 2, dtype=jnp.float32) / ATTN_HEAD_DIM)
    ang = pos.astype(jnp.float32)[:, None] * inv[None, :]
    return jnp.cos(ang), jnp.sin(ang)


def apply_rope(x, cos, sin):
    x1, x2 = jnp.split(x.astype(jnp.float32), 2, axis=-1)
    c = cos[:, None, :]
    s = sin[:, None, :]
    return jnp.concatenate([x1 * c - x2 * s, x1 * s + x2 * c], -1).astype(x.dtype)


def causal_dwconv(x, w, b):
    y = lax.conv_general_dilated(
        x, w[:, None, :].astype(x.dtype), window_strides=(1,), padding=[(w.shape[0] - 1, 0)],
        dimension_numbers=("NWC", "WIO", "NWC"), feature_group_count=x.shape[-1])
    return y + b


def segsum(a):
    cs = jnp.cumsum(a, -1)
    T = a.shape[-1]
    diff = cs[..., :, None] - cs[..., None, :]
    return jnp.where(jnp.tril(jnp.ones((T, T), dtype=bool)), diff, -jnp.inf)


def ssd_chunked(xh, dt, A, Bm, Cm):
    b, L, G, K, P = xh.shape
    N = Bm.shape[-1]
    c, l = L // SSD_CHUNK, SSD_CHUNK
    xdt = (xh.astype(jnp.float32) * dt[..., None]).reshape(b, c, l, G, K, P)
    a = (dt * A).reshape(b, c, l, G, K).transpose(0, 3, 4, 1, 2)
    Bc = Bm.astype(jnp.float32).reshape(b, c, l, G, N)
    Cc = Cm.astype(jnp.float32).reshape(b, c, l, G, N)
    a_cs = jnp.cumsum(a, -1)
    Lmat = jnp.exp(segsum(a))
    CB = jnp.einsum('bclgn,bcsgn->bgcls', Cc, Bc)
    scores = CB[:, :, None] * Lmat
    y_diag = jnp.einsum('bgkcls,bcsgkp->bclgkp', scores, xdt)
    decay_states = jnp.exp(a_cs[..., -1:] - a_cs).transpose(0, 3, 4, 1, 2)
    states = jnp.einsum('bclgn,bclgkp->bcgkpn', Bc, xdt * decay_states[..., None])
    states = jnp.concatenate([jnp.zeros_like(states[:, :1]), states], axis=1)
    last = jnp.pad(a_cs[..., -1], ((0, 0), (0, 0), (0, 0), (1, 0)))
    chunk_decay = jnp.exp(segsum(last))
    new_states = jnp.einsum('bgkzc,bcgkpn->bzgkpn', chunk_decay, states)
    prev_states = new_states[:, :-1]
    out_decay = jnp.exp(a_cs).transpose(0, 3, 4, 1, 2)
    y_off = jnp.einsum('bclgn,bcgkpn->bclgkp', Cc, prev_states) * out_decay[..., None]
    return (y_diag + y_off).reshape(b, L, G, K, P)


def ssd_mixer(z, xbc, dt_raw, conv_w, conv_b, dt_bias, a_log, d_skip, norm_w):
    b, L, _ = z.shape
    xbc = jax.nn.silu(causal_dwconv(xbc, conv_w, conv_b))
    xs, Bm, Cm = jnp.split(xbc, [SSD_WIDTH, SSD_WIDTH + SSD_GROUPS * SSD_STATE], axis=-1)
    xh = xs.reshape(b, L, SSD_GROUPS, SSD_HEADS_PER_GROUP, SSD_HEAD_DIM)
    Bm = Bm.reshape(b, L, SSD_GROUPS, SSD_STATE)
    Cm = Cm.reshape(b, L, SSD_GROUPS, SSD_STATE)
    dt = jax.nn.softplus(dt_raw.astype(jnp.float32) + dt_bias.astype(jnp.float32))
    dt = dt.reshape(b, L, SSD_GROUPS, SSD_HEADS_PER_GROUP)
    A = -jnp.exp(a_log.astype(jnp.float32)).reshape(SSD_GROUPS, SSD_HEADS_PER_GROUP)
    y = ssd_chunked(xh, dt, A, Bm, Cm)
    y = y + d_skip.astype(jnp.float32).reshape(SSD_GROUPS, SSD_HEADS_PER_GROUP)[..., None] * xh.astype(jnp.float32)
    gw = SSD_HEADS_PER_GROUP * SSD_HEAD_DIM
    y = y.reshape(b, L, SSD_GROUPS, gw) * jax.nn.silu(z.astype(jnp.float32)).reshape(b, L, SSD_GROUPS, gw)
    y = rms_norm(y, norm_w.reshape(SSD_GROUPS, gw))
    return y.reshape(b, L, SSD_WIDTH)


def compress_blocks(k, cmp_idx, pos_emb, w1, w2):
    b, _, hk, dh = k.shape
    n = cmp_idx.shape[0]
    blocks = k[:, cmp_idx] + pos_emb[:, None, :]
    flat = blocks.transpose(0, 1, 3, 2, 4).reshape(b, n, hk, CMP_BLOCK * dh)
    return jax.nn.gelu(flat @ w1, approximate=False) @ w2


def nsa_mixer(q, k_cmp, v_cmp, k_slc, v_slc, k_win, v_win, gate_raw,
              cmp_pos_k, cmp_w1_k, cmp_w2_k, cmp_pos_v, cmp_w1_v, cmp_w2_v):
    b, L, _ = q.shape
    n_cmp = (L - CMP_BLOCK) // CMP_STRIDE + 1
    n_slc = L // SLC_BLOCK
    n_sel = min(SLC_TOPK, n_slc)
    n_qb = L // Q_BLOCK
    kv = lambda t: t.reshape(b, L, KV_HEADS, ATTN_HEAD_DIM)

    cos, sin = rope_angles(jnp.arange(L))
    qh = apply_rope(q.reshape(b, L, ATTN_HEADS, ATTN_HEAD_DIM), cos, sin) * (ATTN_HEAD_DIM ** -0.5)
    qh = qh.reshape(b, L, KV_HEADS, Q_PER_KV, ATTN_HEAD_DIM)
    gates = jax.nn.sigmoid(gate_raw.astype(jnp.float32)).reshape(b, L, KV_HEADS, Q_PER_KV, N_GATES)

    cmp_idx = np.arange(n_cmp)[:, None] * CMP_STRIDE + np.arange(CMP_BLOCK)[None, :]
    cmp_end_np = cmp_idx[:, -1]
    ccos, csin = rope_angles(jnp.asarray(cmp_end_np))
    kc = apply_rope(compress_blocks(kv(k_cmp), cmp_idx, cmp_pos_k, cmp_w1_k, cmp_w2_k), ccos, csin)
    vc = compress_blocks(kv(v_cmp), cmp_idx, cmp_pos_v, cmp_w1_v, cmp_w2_v)
    cmp_end = jnp.asarray(cmp_end_np)
    cs = np.arange(n_cmp) * CMP_STRIDE
    ss = np.arange(n_slc) * SLC_BLOCK
    overlap = jnp.asarray(((cs[:, None] < ss[None, :] + SLC_BLOCK) &
                           (cs[:, None] + CMP_BLOCK > ss[None, :])).astype(np.float32))

    ks = apply_rope(kv(k_slc), cos, sin)
    ks_blk = ks.reshape(b, n_slc, SLC_BLOCK, KV_HEADS, ATTN_HEAD_DIM).transpose(0, 3, 1, 2, 4)
    vs_blk = kv(v_slc).reshape(b, n_slc, SLC_BLOCK, KV_HEADS, ATTN_HEAD_DIM).transpose(0, 3, 1, 2, 4)
    gather_blocks = jax.vmap(jax.vmap(lambda kb, ii: kb[ii]))

    pad = ((0, 0), (WINDOW, 0), (0, 0), (0, 0))
    kw_pad = jnp.pad(apply_rope(kv(k_win), cos, sin), pad)
    vw_pad = jnp.pad(kv(v_win), pad)
    blk = jnp.arange(n_slc)

    def block(qi):
        start = qi * Q_BLOCK
        t = start + jnp.arange(Q_BLOCK)
        qb = lax.dynamic_slice_in_dim(qh, start, Q_BLOCK, axis=1)
        gb = lax.dynamic_slice_in_dim(gates, start, Q_BLOCK, axis=1)

        s_c = jnp.einsum('bqhgd,bnhd->bhgqn', qb, kc).astype(jnp.float32)
        valid_c = cmp_end[None, :] <= t[:, None]
        p_c = jnp.where(valid_c, jax.nn.softmax(jnp.where(valid_c, s_c, NEG_INF), -1), 0.0)
        o_c = jnp.einsum('bhgqn,bnhd->bqhgd', p_c, vc.astype(jnp.float32))

        imp = jnp.einsum('bhgqn,nj->bhqj', p_c, overlap)
        tb = t // SLC_BLOCK
        causal_b = blk[None, :] * SLC_BLOCK <= t[:, None]
        forced = (blk[None, :] == 0) | (blk[None, :] == tb[:, None]) | (blk[None, :] == tb[:, None] - 1)
        imp = jnp.where(causal_b, jnp.where(forced, BIG, imp), NEG_INF)
        top_s, top_i = lax.top_k(imp, n_sel)
        blk_ok = top_s > 0.5 * NEG_INF
        kg = gather_blocks(ks_blk, top_i)
        vg = gather_blocks(vs_blk, top_i)
        s_s = jnp.einsum('bqhgd,bhqjld->bhgqjl', qb, kg).astype(jnp.float32)
        key_pos = top_i[..., None] * SLC_BLOCK + jnp.arange(SLC_BLOCK)
        mask_s = blk_ok[..., None] & (key_pos <= t[None, None, :, None, None])
        s_s = jnp.where(mask_s[:, :, None], s_s, NEG_INF)
        sh = s_s.shape
        p_s = jax.nn.softmax(s_s.reshape(sh[:4] + (n_sel * SLC_BLOCK,)), -1).reshape(sh)
        o_s = jnp.einsum('bhgqjl,bhqjld->bqhgd', p_s, vg.astype(jnp.float32))

        kw_b = lax.dynamic_slice_in_dim(kw_pad, start, Q_BLOCK + WINDOW, axis=1)
        vw_b = lax.dynamic_slice_in_dim(vw_pad, start, Q_BLOCK + WINDOW, axis=1)
        kpos = start - WINDOW + jnp.arange(Q_BLOCK + WINDOW)
        mask_w = (kpos[None, :] <= t[:, None]) & (kpos[None, :] > t[:, None] - WINDOW) & (kpos[None, :] >= 0)
        s_w = jnp.einsum('bqhgd,bkhd->bhgqk', qb, kw_b).astype(jnp.float32)
        p_w = jax.nn.softmax(jnp.where(mask_w, s_w, NEG_INF), -1)
        o_w = jnp.einsum('bhgqk,bkhd->bqhgd', p_w, vw_b.astype(jnp.float32))

        return gb[..., 0:1] * o_c + gb[..., 1:2] * o_s + gb[..., 2:3] * o_w

    out = lax.map(block, jnp.arange(n_qb))
    return out.transpose(1, 0, 2, 3, 4, 5).reshape(b, L, ATTN_WIDTH)


def peer_ffn(x, w_query, sub_keys, expert_u, expert_v):
    b, L, D = x.shape
    T = b * L
    xt = x.reshape(T, D)
    q = (xt @ w_query).reshape(T, PEER_HEADS, 2, PEER_KEY_DIM // 2)
    s = jnp.einsum('thcd,hckd->thck', q, sub_keys).astype(jnp.float32)
    s_top, i_top = lax.top_k(s, PEER_TOPK)
    cand = (s_top[:, :, 0, :, None] + s_top[:, :, 1, None, :]).reshape(T, PEER_HEADS, PEER_TOPK * PEER_TOPK)
    cand_idx = (i_top[:, :, 0, :, None] * PEER_KEYS + i_top[:, :, 1, None, :]).reshape(T, PEER_HEADS, PEER_TOPK * PEER_TOPK)
    g_s, g_pos = lax.top_k(cand, PEER_TOPK)
    e_idx = jnp.take_along_axis(cand_idx, g_pos, axis=-1)
    g = jax.nn.softmax(g_s, -1)
    nb = T // PEER_TOKEN_BLOCK

    def block(args):
        xb, eb, gb = args
        u = expert_u[eb]
        h = jax.nn.gelu(jnp.einsum('td,thkd->thk', xb, u).astype(jnp.float32), approximate=False)
        v = expert_v[eb]
        return jnp.einsum('thk,thkd->td', (gb * h).astype(v.dtype), v)

    y = lax.map(block, (xt.reshape(nb, PEER_TOKEN_BLOCK, D),
                        e_idx.reshape(nb, PEER_TOKEN_BLOCK, PEER_HEADS, PEER_TOPK),
                        g.reshape(nb, PEER_TOKEN_BLOCK, PEER_HEADS, PEER_TOPK)))
    return y.reshape(b, L, D).astype(x.dtype)


def setup_inputs(seed: int = 0) -> dict:
    key = jax.random.key(seed)
    ks = jax.random.split(key, 24)
    f32 = jnp.float32
    nl = DEPTH

    def nrm(k, shape, scale):
        return jax.random.normal(k, shape, f32) * scale

    x = nrm(ks[0], (BATCH, SEQ, D_MODEL), 1.0)
    w_in = nrm(ks[1], (nl, D_MODEL, IN_PROJ_DIM), D_MODEL ** -0.5)
    conv_w = nrm(ks[2], (nl, SSD_CONV, SSD_CONV_DIM), SSD_CONV ** -0.5)
    conv_b = nrm(ks[3], (nl, SSD_CONV_DIM), 0.01)
    dt0 = jnp.exp(jax.random.uniform(ks[4], (nl, SSD_HEADS), f32, math.log(1e-3), math.log(1e-1)))
    dt_bias = dt0 + jnp.log(-jnp.expm1(-dt0))
    a_log = jnp.log(jax.random.uniform(ks[5], (nl, SSD_HEADS), f32, 1.0, 16.0))
    d_skip = 1.0 + nrm(ks[6], (nl, SSD_HEADS), 0.1)
    ssd_norm_w = 1.0 + nrm(ks[7], (nl, SSD_WIDTH), 0.02)
    cmp_in = CMP_BLOCK * ATTN_HEAD_DIM
    cmp_pos_k = nrm(ks[8], (nl, CMP_BLOCK, ATTN_HEAD_DIM), 0.02)
    cmp_w1_k = nrm(ks[9], (nl, cmp_in, CMP_HIDDEN), cmp_in ** -0.5)
    cmp_w2_k = nrm(ks[10], (nl, CMP_HIDDEN, ATTN_HEAD_DIM), CMP_HIDDEN ** -0.5)
    cmp_pos_v = nrm(ks[11], (nl, CMP_BLOCK, ATTN_HEAD_DIM), 0.02)
    cmp_w1_v = nrm(ks[12], (nl, cmp_in, CMP_HIDDEN), cmp_in ** -0.5)
    cmp_w2_v = nrm(ks[13], (nl, CMP_HIDDEN, ATTN_HEAD_DIM), CMP_HIDDEN ** -0.5)
    attn_norm_w = 1.0 + nrm(ks[14], (nl, ATTN_WIDTH), 0.02)
    w_out = nrm(ks[15], (nl, MIX_WIDTH, D_MODEL), (MIX_WIDTH ** -0.5) * DN_BETA)
    ln1_g = 1.0 + nrm(ks[16], (nl, D_MODEL), 0.02)
    ln1_b = nrm(ks[17], (nl, D_MODEL), 0.01)
    peer_w_query = nrm(ks[18], (nl, D_MODEL, PEER_HEADS * PEER_KEY_DIM), D_MODEL ** -0.5)
    peer_sub_keys = nrm(ks[19], (nl, PEER_HEADS, 2, PEER_KEYS, PEER_KEY_DIM // 2), (PEER_KEY_DIM // 2) ** -0.5)
    peer_u = nrm(ks[20], (nl, PEER_EXPERTS, D_MODEL), D_MODEL ** -0.5)
    peer_v = nrm(ks[21], (nl, PEER_EXPERTS, D_MODEL), DN_BETA)
    ln2_g = 1.0 + nrm(ks[22], (nl, D_MODEL), 0.02)
    ln2_b = nrm(ks[23], (nl, D_MODEL), 0.01)
    return {"x": x, "w_in": w_in, "conv_w": conv_w, "conv_b": conv_b, "dt_bias": dt_bias,
            "a_log": a_log, "d_skip": d_skip, "ssd_norm_w": ssd_norm_w,
            "cmp_pos_k": cmp_pos_k, "cmp_w1_k": cmp_w1_k, "cmp_w2_k": cmp_w2_k,
            "cmp_pos_v": cmp_pos_v, "cmp_w1_v": cmp_w1_v, "cmp_w2_v": cmp_w2_v,
            "attn_norm_w": attn_norm_w, "w_out": w_out, "ln1_g": ln1_g, "ln1_b": ln1_b,
            "peer_w_query": peer_w_query, "peer_sub_keys": peer_sub_keys,
            "peer_u": peer_u, "peer_v": peer_v, "ln2_g": ln2_g, "ln2_b": ln2_b}


def reference(x, w_in, conv_w, conv_b, dt_bias, a_log, d_skip, ssd_norm_w,
              cmp_pos_k, cmp_w1_k, cmp_w2_k, cmp_pos_v, cmp_w1_v, cmp_w2_v,
              attn_norm_w, w_out, ln1_g, ln1_b, peer_w_query, peer_sub_keys,
              peer_u, peer_v, ln2_g, ln2_b):
    h = x
    for i in range(DEPTH):
        proj = h @ w_in[i]
        z, xbc, dt_raw, q, kvs, gate_raw = jnp.split(proj, SPLIT_POINTS, axis=-1)
        k_cmp, v_cmp, k_slc, v_slc, k_win, v_win = jnp.split(kvs, 6, axis=-1)
        y_ssd = ssd_mixer(z, xbc, dt_raw, conv_w[i], conv_b[i], dt_bias[i], a_log[i], d_skip[i], ssd_norm_w[i])
        y_att = nsa_mixer(q, k_cmp, v_cmp, k_slc, v_slc, k_win, v_win, gate_raw,
                          cmp_pos_k[i], cmp_w1_k[i], cmp_w2_k[i], cmp_pos_v[i], cmp_w1_v[i], cmp_w2_v[i])
        y_att = rms_norm(y_att, attn_norm_w[i])
        mix = jnp.concatenate([y_ssd, y_att], axis=-1).astype(h.dtype) @ w_out[i]
        h = layer_norm(DN_ALPHA * h + mix, ln1_g[i], ln1_b[i])
        ffn = peer_ffn(h, peer_w_query[i], peer_sub_keys[i], peer_u[i], peer_v[i])
        h = layer_norm(DN_ALPHA * h + ffn, ln2_g[i], ln2_b[i])
    return h
```

```python
import functools
import math

import jax
import jax.numpy as jnp
import numpy as np
from jax import lax
from jax.experimental import pallas as pl
from jax.experimental.pallas import tpu as pltpu

F32 = jnp.float32
BF16 = jnp.bfloat16

D_MODEL = 1024
SSD_WIDTH = 1024
SSD_HEAD_DIM = 64
SSD_HEADS = 16
SSD_GROUPS = 2
SSD_STATE = 128
SSD_CONV = 4
SSD_CHUNK = 128
SSD_CONV_DIM = SSD_WIDTH + 2 * SSD_GROUPS * SSD_STATE
ATTN_WIDTH = 1024
HEAD_DIM = 64
ATTN_HEADS = 16
KV_HEADS = 4
Q_PER_KV = 4
KV_WIDTH = KV_HEADS * HEAD_DIM
CMP_BLOCK = 32
CMP_STRIDE = 16
CMP_HIDDEN = 256
SLC_BLOCK = 64
SLC_TOPK = 16
WINDOW = 512
ROPE_THETA = 10000.0
N_GATES = 3
PEER_HEADS = 8
PEER_KEYS = 128
PEER_EXPERTS = PEER_KEYS * PEER_KEYS
PEER_TOPK = 16
DEPTH = 1
DN_ALPHA = (2 * DEPTH) ** 0.25
LN_EPS = 1e-5
RMS_EPS = 1e-5
NEG_INF = -1e30
BIG = 1e30

LANES = 128
SUBLANES = 8
VMEM_LIMIT_BYTES = 56 * 1024 * 1024

_OFF_Z = 0
_OFF_XBC = _OFF_Z + SSD_WIDTH
_OFF_DT = _OFF_XBC + SSD_CONV_DIM
_OFF_Q = _OFF_DT + SSD_HEADS
_OFF_KV = _OFF_Q + ATTN_WIDTH
_OFF_GATE = _OFF_KV + 6 * KV_WIDTH
_GATE_LANE0 = SSD_HEADS

_NT = (((1,), (1,)), ((), ()))
_TN = (((0,), (0,)), ((), ()))


def _dot(a, b):
    return jnp.dot(a, b, preferred_element_type=F32)


def _dot_nt(a, b):
    return lax.dot_general(a, b, _NT, preferred_element_type=F32)


def _dot_tn(a, b):
    return lax.dot_general(a, b, _TN, preferred_element_type=F32)


def _gelu_erf(x):
    return 0.5 * x * (1.0 + lax.erf(x * (1.0 / math.sqrt(2.0))))


def _layer_norm_rows(r, g, b):
    mu = jnp.mean(r, axis=-1, keepdims=True)
    d = r - mu
    var = jnp.mean(d * d, axis=-1, keepdims=True)
    return d * lax.rsqrt(var + LN_EPS) * g + b


def _params(sem):
    return pltpu.CompilerParams(dimension_semantics=sem, vmem_limit_bytes=VMEM_LIMIT_BYTES)


_INPROJ_TM = 256
_NCHUNK = 256


def _rope_t(acc, c, s, scale):
    outs = []
    for hd in range(acc.shape[0] // HEAD_DIM):
        x1 = acc[hd * HEAD_DIM: hd * HEAD_DIM + HEAD_DIM // 2]
        x2 = acc[hd * HEAD_DIM + HEAD_DIM // 2: (hd + 1) * HEAD_DIM]
        outs.append((x1 * c - x2 * s) * scale)
        outs.append((x1 * s + x2 * c) * scale)
    return jnp.concatenate(outs, axis=0)


def _inproj_kernel(x_ref, wn_ref, wt_ref, cos_ref, sin_ref,
                   z_ref, xbc_ref, dtg_ref, kvc_ref, qt_ref, kvt_ref):
    xb = x_ref[0].astype(BF16)
    col = 0
    for o_ref in (z_ref, xbc_ref, dtg_ref, kvc_ref):
        width = o_ref.shape[-1]
        for c0 in range(0, width, _NCHUNK):
            c1 = min(c0 + _NCHUNK, width)
            o_ref[0, :, c0:c1] = _dot(xb, wn_ref[:, col + c0: col + c1])
        col += width
    c = cos_ref[...]
    s = sin_ref[...]
    q_scale = HEAD_DIM ** -0.5
    n_q_chunks = ATTN_WIDTH // _NCHUNK
    for ci in range(n_q_chunks):
        acc = _dot_nt(wt_ref[ci * _NCHUNK:(ci + 1) * _NCHUNK, :], xb)
        qt_ref[0, ci * _NCHUNK:(ci + 1) * _NCHUNK, :] = _rope_t(acc, c, s, q_scale).astype(BF16)
    for ci in range(4):
        r0 = ATTN_WIDTH + ci * KV_WIDTH
        acc = _dot_nt(wt_ref[r0:r0 + KV_WIDTH, :], xb)
        if ci % 2 == 0:
            acc = _rope_t(acc, c, s, 1.0)
        kvt_ref[0, ci * KV_WIDTH:(ci + 1) * KV_WIDTH, :] = acc.astype(BF16)


def _in_projection(x, w_in):
    B, L, D = x.shape
    tm = min(_INPROJ_TM, L)
    w = w_in
    dtg_w = jnp.concatenate([w[:, _OFF_DT:_OFF_DT + SSD_HEADS], w[:, _OFF_GATE:],
                             jnp.zeros((D, LANES - SSD_HEADS - ATTN_HEADS * N_GATES), w.dtype)], axis=1)
    wn = jnp.concatenate([w[:, _OFF_Z:_OFF_DT], dtg_w, w[:, _OFF_KV:_OFF_KV + 2 * KV_WIDTH]], axis=1).astype(BF16)
    wt = jnp.concatenate([w[:, _OFF_Q:_OFF_KV], w[:, _OFF_KV + 2 * KV_WIDTH:_OFF_GATE]], axis=1).T.astype(BF16)
    inv = ROPE_THETA ** (-jnp.arange(0, HEAD_DIM, 2, dtype=F32) / HEAD_DIM)
    ang = inv[:, None] * jnp.arange(L, dtype=F32)[None, :]
    cos_t, sin_t = jnp.cos(ang), jnp.sin(ang)
    n_nat = wn.shape[1]
    out_shape = (
        jax.ShapeDtypeStruct((B, L, SSD_WIDTH), F32),
        jax.ShapeDtypeStruct((B, L, SSD_CONV_DIM), F32),
        jax.ShapeDtypeStruct((B, L, LANES), F32),
        jax.ShapeDtypeStruct((B, L, 2 * KV_WIDTH), F32),
        jax.ShapeDtypeStruct((B, ATTN_WIDTH, L), BF16),
        jax.ShapeDtypeStruct((B, 4 * KV_WIDTH, L), BF16),
    )
    nat = lambda wdt: pl.BlockSpec((1, tm, wdt), lambda b, i: (b, i, 0))
    tr = lambda rows: pl.BlockSpec((1, rows, tm), lambda b, i: (b, 0, i))
    return pl.pallas_call(
        _inproj_kernel,
        grid=(B, L // tm),
        in_specs=[
            pl.BlockSpec((1, tm, D), lambda b, i: (b, i, 0)),
            pl.BlockSpec((D, n_nat), lambda b, i: (0, 0)),
            pl.BlockSpec((wt.shape[0], D), lambda b, i: (0, 0)),
            pl.BlockSpec((HEAD_DIM // 2, tm), lambda b, i: (0, i)),
            pl.BlockSpec((HEAD_DIM // 2, tm), lambda b, i: (0, i)),
        ],
        out_specs=(nat(SSD_WIDTH), nat(SSD_CONV_DIM), nat(LANES), nat(2 * KV_WIDTH),
                   tr(ATTN_WIDTH), tr(4 * KV_WIDTH)),
        out_shape=out_shape,
        compiler_params=_params(("parallel", "parallel")),
        name="in_projection",
    )(x, wn, wt, cos_t, sin_t)


_CONV_HALO = SUBLANES


def _ssd_kernel(xbc_ref, z_ref, dtg_ref, cw_ref, cb_ref, dtb_ref, alog_ref, dsk_ref, nw_ref,
                y_ref, ext_sc, state_sc, y_sc):
    cidx = pl.program_id(1)
    T = SSD_CHUNK

    @pl.when(cidx == 0)
    def _():
        ext_sc[0:_CONV_HALO, :] = jnp.zeros((_CONV_HALO, SSD_CONV_DIM), F32)
        state_sc[...] = jnp.zeros(state_sc.shape, F32)

    ext_sc[_CONV_HALO:_CONV_HALO + T, :] = xbc_ref[0]
    conv = cb_ref[...]
    for k in range(SSD_CONV):
        off = _CONV_HALO - (SSD_CONV - 1) + k
        conv = conv + cw_ref[k:k + 1, :] * ext_sc[pl.ds(off, T), :]
    ext_sc[0:_CONV_HALO, :] = xbc_ref[0, T - _CONV_HALO:T, :]
    u = conv * jax.nn.sigmoid(conv)

    lane = lax.broadcasted_iota(jnp.int32, (1, LANES), 1)
    head_lane = lane < SSD_HEADS
    dt_nat = jax.nn.softplus(dtg_ref[0] + dtb_ref[...])
    a_row = jnp.where(head_lane, -jnp.exp(alog_ref[...]), 0.0)
    a_nat = dt_nat * a_row
    ri = lax.broadcasted_iota(jnp.int32, (T, T), 0)
    ci = lax.broadcasted_iota(jnp.int32, (T, T), 1)
    tril = ri >= ci
    cs_nat = jnp.dot(tril.astype(F32), a_nat, precision=lax.Precision.HIGHEST,
                     preferred_element_type=F32)
    cs_t = cs_nat.T
    cs_last = cs_nat[T - 1:T, :]
    left = lane < SSD_HEAD_DIM

    heads_per_group = SSD_HEADS // SSD_GROUPS
    pairs_per_group = heads_per_group // 2
    for g in range(SSD_GROUPS):
        b0 = SSD_WIDTH + g * SSD_STATE
        c0 = SSD_WIDTH + SSD_GROUPS * SSD_STATE + g * SSD_STATE
        bm = u[:, b0:b0 + SSD_STATE].astype(BF16)
        cm = u[:, c0:c0 + SSD_STATE].astype(BF16)
        cb = _dot_nt(cm, bm)
        for pp in range(pairs_per_group):
            pr = g * pairs_per_group + pp
            h0, h1 = 2 * pr, 2 * pr + 1
            csb0 = jnp.broadcast_to(cs_nat[:, h0:h0 + 1], (T, LANES))
            csb1 = jnp.broadcast_to(cs_nat[:, h1:h1 + 1], (T, LANES))
            lm0 = jnp.where(tril, jnp.exp(csb0 - cs_t[h0:h0 + 1, :]), 0.0)
            lm1 = jnp.where(tril, jnp.exp(csb1 - cs_t[h1:h1 + 1, :]), 0.0)
            csx = jnp.where(left, csb0, csb1)
            dtx = jnp.where(left, jnp.broadcast_to(dt_nat[:, h0:h0 + 1], (T, LANES)),
                            jnp.broadcast_to(dt_nat[:, h1:h1 + 1], (T, LANES)))
            last_x = jnp.where(left, cs_last[:, h0:h0 + 1], cs_last[:, h1:h1 + 1])
            xs_p = u[:, pr * LANES:(pr + 1) * LANES]
            xdt = xs_p * dtx
            scores = jnp.concatenate([cb * lm0, cb * lm1], axis=1).astype(BF16)
            rhs = jnp.concatenate([jnp.where(left, xdt, 0.0), jnp.where(left, 0.0, xdt)],
                                  axis=0).astype(BF16)
            y_diag = _dot(scores, rhs)
            st_prev = state_sc[pr]
            y_off = _dot(cm, st_prev.astype(BF16)) * jnp.exp(csx)
            st_new = _dot_tn(bm, (xdt * jnp.exp(last_x - csx)).astype(BF16))
            state_sc[pr] = jnp.exp(last_x) * st_prev + st_new
            y_sc[:, pr * LANES:(pr + 1) * LANES] = (
                y_diag + y_off + dsk_ref[:, pr * LANES:(pr + 1) * LANES] * xs_p)

    zz = z_ref[0]
    gw = SSD_WIDTH // SSD_GROUPS
    for g in range(SSD_GROUPS):
        zg = zz[:, g * gw:(g + 1) * gw]
        yg = y_sc[:, g * gw:(g + 1) * gw] * (zg * jax.nn.sigmoid(zg))
        ms = jnp.mean(yg * yg, axis=-1, keepdims=True)
        y_ref[0, :, g * gw:(g + 1) * gw] = (
            yg * lax.rsqrt(ms + RMS_EPS) * nw_ref[:, g * gw:(g + 1) * gw]).astype(BF16)


def _ssd_mixer(z, xbc, dtg, conv_w, conv_b, dt_bias, a_log, d_skip, norm_w):
    B, L, _ = z.shape
    T = SSD_CHUNK
    pad16 = lambda v: jnp.pad(v.astype(F32), (0, LANES - SSD_HEADS)).reshape(1, LANES)
    dsk_x = jnp.repeat(d_skip.astype(F32), SSD_HEAD_DIM).reshape(1, SSD_WIDTH)
    const = lambda shape: pl.BlockSpec(shape, lambda b, c: (0,) * len(shape))
    return pl.pallas_call(
        _ssd_kernel,
        grid=(B, L // T),
        in_specs=[
            pl.BlockSpec((1, T, SSD_CONV_DIM), lambda b, c: (b, c, 0)),
            pl.BlockSpec((1, T, SSD_WIDTH), lambda b, c: (b, c, 0)),
            pl.BlockSpec((1, T, LANES), lambda b, c: (b, c, 0)),
            const((SSD_CONV, SSD_CONV_DIM)), const((1, SSD_CONV_DIM)),
            const((1, LANES)), const((1, LANES)), const((1, SSD_WIDTH)), const((1, SSD_WIDTH)),
        ],
        out_specs=pl.BlockSpec((1, T, SSD_WIDTH), lambda b, c: (b, c, 0)),
        out_shape=jax.ShapeDtypeStruct((B, L, SSD_WIDTH), BF16),
        scratch_shapes=[
            pltpu.VMEM((_CONV_HALO + T, SSD_CONV_DIM), F32),
            pltpu.VMEM((SSD_HEADS // 2, SSD_STATE, LANES), F32),
            pltpu.VMEM((T, SSD_WIDTH), F32),
        ],
        compiler_params=_params(("parallel", "arbitrary")),
        name="ssd_mixer",
    )(xbc, z, dtg, conv_w, conv_b.reshape(1, -1), pad16(dt_bias), pad16(a_log), dsk_x,
      norm_w.reshape(1, -1))


def _compress_kernel(kv_ref, pos_ref, w1_ref, w2_ref, w2r_ref, cos_ref, sin_ref,
                     kc_ref, vct_ref, shift_sc):
    nseg = kv_ref.shape[1]
    for which in range(2):
        for h in range(KV_HEADS):
            lane0 = which * KV_WIDTH + h * HEAD_DIM
            acc_a = jnp.zeros((nseg, CMP_HIDDEN), F32)
            acc_b = jnp.zeros((nseg, CMP_HIDDEN), F32)
            for l in range(CMP_STRIDE):
                xl = kv_ref[0, :, l, lane0:lane0 + HEAD_DIM]
                xa = (xl + pos_ref[which, l:l + 1, :]).astype(BF16)
                xb = (xl + pos_ref[which, CMP_STRIDE + l:CMP_STRIDE + l + 1, :]).astype(BF16)
                acc_a = acc_a + _dot(xa, w1_ref[which, l * HEAD_DIM:(l + 1) * HEAD_DIM, :])
                acc_b = acc_b + _dot(
                    xb, w1_ref[which, (CMP_STRIDE + l) * HEAD_DIM:(CMP_STRIDE + l + 1) * HEAD_DIM, :])
            shift_sc[0:nseg, :] = acc_b
            shift_sc[nseg:nseg + SUBLANES, :] = jnp.zeros((SUBLANES, CMP_HIDDEN), F32)
            pre = acc_a + shift_sc[pl.ds(1, nseg), :]
            hid = _gelu_erf(pre).astype(BF16)
            out = _dot(hid, w2_ref[which])
            if which == 0:
                out_r = _dot(hid, w2r_ref[...])
                kc_ref[0, h] = (out * cos_ref[...] + out_r * sin_ref[...]).astype(BF16)
            else:
                vct_ref[0, h] = out.T.astype(BF16)


def _compress(kvc, pos_k, w1_k, w2_k, pos_v, w1_v, w2_v):
    B, L, _ = kvc.shape
    nseg = L // CMP_STRIDE
    kv4 = kvc.reshape(B, nseg, CMP_STRIDE, 2 * KV_WIDTH)
    pos = jnp.stack([pos_k, pos_v]).astype(F32)
    w1 = jnp.stack([w1_k, w1_v]).astype(BF16)
    w2 = jnp.stack([w2_k, w2_v]).astype(BF16)
    half = HEAD_DIM // 2
    w2r = jnp.concatenate([-w2_k[:, half:], w2_k[:, :half]], axis=1).astype(BF16)
    inv = ROPE_THETA ** (-jnp.arange(0, HEAD_DIM, 2, dtype=F32) / HEAD_DIM)
    end = (jnp.arange(nseg) * CMP_STRIDE + CMP_BLOCK - 1).astype(F32)
    ang = end[:, None] * inv[None, :]
    cos_n = jnp.concatenate([jnp.cos(ang)] * 2, axis=1)
    sin_n = jnp.concatenate([jnp.sin(ang)] * 2, axis=1)
    const = lambda shape: pl.BlockSpec(shape, lambda b: (0,) * len(shape))
    return pl.pallas_call(
        _compress_kernel,
        grid=(B,),
        in_specs=[
            pl.BlockSpec((1, nseg, CMP_STRIDE, 2 * KV_WIDTH), lambda b: (b, 0, 0, 0)),
            const(pos.shape), const(w1.shape), const(w2.shape), const(w2r.shape),
            const(cos_n.shape), const(sin_n.shape),
        ],
        out_specs=(pl.BlockSpec((1, KV_HEADS, nseg, HEAD_DIM), lambda b: (b, 0, 0, 0)),
                   pl.BlockSpec((1, KV_HEADS, HEAD_DIM, nseg), lambda b: (b, 0, 0, 0))),
        out_shape=(jax.ShapeDtypeStruct((B, KV_HEADS, nseg, HEAD_DIM), BF16),
                   jax.ShapeDtypeStruct((B, KV_HEADS, HEAD_DIM, nseg), BF16)),
        scratch_shapes=[pltpu.VMEM((nseg + SUBLANES, CMP_HIDDEN), F32)],
        compiler_params=_params(("parallel",)),
        name="nsa_compress",
    )(kv4, pos, w1, w2, w2r, cos_n, sin_n)


_TQ = 128
_KB = 128


def _attn_kernel(qt_ref, kst_ref, vst_ref, kwt_ref, vwt_ref, kc_ref, vct_ref, dtg_ref, ov_ref,
                 o_ref, ks_sc, kw_sc, sel_sc, gate_sc):
    h = pl.program_id(1)
    qi = pl.program_id(2)
    nq = Q_PER_KV * _TQ

    @pl.when(qi == 0)
    def _():
        ks_sc[...] = kst_ref[0].T
        kw_sc[...] = kwt_ref[0].T

    q4 = qt_ref[0]
    qc = jnp.concatenate([q4[g * HEAD_DIM:(g + 1) * HEAD_DIM, :] for g in range(Q_PER_KV)], axis=1)
    lane = lax.broadcasted_iota(jnp.int32, (1, nq), 1)
    t_row = qi * _TQ + (lane & (_TQ - 1))
    t128 = qi * _TQ + lax.broadcasted_iota(jnp.int32, (1, _TQ), 1)

    gate_sc[...] = jax.nn.sigmoid(dtg_ref[0]).T

    def gate_row(j):
        rows = [gate_sc[pl.ds(_GATE_LANE0 + (h * Q_PER_KV + g) * N_GATES + j, 1), :]
                for g in range(Q_PER_KV)]
        return jnp.concatenate(rows, axis=1)

    kc = kc_ref[0, 0]
    ncmp = kc.shape[0]
    s_c = _dot(kc, qc)
    cend = lax.broadcasted_iota(jnp.int32, (ncmp, 1), 0) * CMP_STRIDE + (CMP_BLOCK - 1)
    valid = cend <= t_row
    s_c = jnp.where(valid, s_c, NEG_INF)
    m_c = jnp.max(s_c, axis=0, keepdims=True)
    e_c = jnp.exp(s_c - m_c)
    p_c = jnp.where(valid, e_c / jnp.sum(e_c, axis=0, keepdims=True), 0.0).astype(BF16)
    o_c = _dot(vct_ref[0, 0], p_c)

    p_stack = jnp.concatenate([p_c[:, g * _TQ:(g + 1) * _TQ] for g in range(Q_PER_KV)], axis=0)
    imp = _dot(ov_ref[...], p_stack)
    nslc = imp.shape[0]
    j_col = lax.broadcasted_iota(jnp.int32, (nslc, 1), 0)
    tb = jnp.right_shift(t128, int(math.log2(SLC_BLOCK)))
    causal_b = j_col <= tb
    forced = (j_col == 0) | (j_col == tb) | (j_col == tb - 1)
    imp = jnp.where(causal_b, jnp.where(forced, BIG, imp), NEG_INF)
    rank = jnp.zeros((nslc, _TQ), F32)
    for jp in range(nslc):
        row = imp[jp:jp + 1, :]
        ge = jnp.where(row >= imp, 1.0, 0.0)
        gt = jnp.where(row > imp, 1.0, 0.0)
        rank = rank + jnp.where(j_col > jp, ge, gt)
    sel_sc[...] = jnp.where((rank < float(SLC_TOPK)) & causal_b, 1.0, 0.0)

    key_off = lax.broadcasted_iota(jnp.int32, (_KB, 1), 0)

    def flash_step(j2, carry, k_sc, vt_ref, use_sel):
        m, l, acc = carry
        start = pl.multiple_of(j2 * _KB, _KB)
        s = _dot(k_sc[pl.ds(start, _KB), :], qc)
        kpos = j2 * _KB + key_off
        ok = kpos <= t_row
        if use_sel:
            halves = [jnp.broadcast_to(sel_sc[pl.ds(2 * j2 + i, 1), :], (SLC_BLOCK, _TQ))
                      for i in range(_KB // SLC_BLOCK)]
            sel = jnp.concatenate(halves, axis=0)
            sel = jnp.concatenate([sel] * Q_PER_KV, axis=1)
            ok = ok & (sel > 0.5)
        else:
            ok = ok & (kpos > t_row - WINDOW)
        s = jnp.where(ok, s, NEG_INF)
        m_new = jnp.maximum(m, jnp.max(s, axis=0, keepdims=True))
        alpha = jnp.exp(m - m_new)
        p = jnp.exp(s - m_new)
        l = alpha * l + jnp.sum(p, axis=0, keepdims=True)
        acc = alpha * acc + _dot(vt_ref[0, :, pl.ds(start, _KB)], p.astype(BF16))
        return m_new, l, acc

    init = (jnp.full((1, nq), NEG_INF, F32), jnp.zeros((1, nq), F32), jnp.zeros((HEAD_DIM, nq), F32))

    _, l_s, acc_s = lax.fori_loop(
        0, qi + 1, lambda j2, c: flash_step(j2, c, ks_sc, vst_ref, True), init)
    o_s = acc_s / l_s

    first = jnp.maximum(qi - WINDOW // _KB, 0)
    _, l_w, acc_w = lax.fori_loop(
        first, qi + 1, lambda j2, c: flash_step(j2, c, kw_sc, vwt_ref, False), init)
    o_w = acc_w / l_w

    res = gate_row(0) * o_c + gate_row(1) * o_s + gate_row(2) * o_w
    for g in range(Q_PER_KV):
        o_ref[0, g * HEAD_DIM:(g + 1) * HEAD_DIM, :] = res[:, g * _TQ:(g + 1) * _TQ]


def _nsa_attention(qt, kvt, kc, vct, dtg):
    B, _, L = qt.shape
    ncmp = kc.shape[2]
    nslc = L // SLC_BLOCK
    cs = np.arange(ncmp) * CMP_STRIDE
    ss = np.arange(nslc) * SLC_BLOCK
    ov = ((cs[None, :] < ss[:, None] + SLC_BLOCK) & (cs[None, :] + CMP_BLOCK > ss[:, None]))
    ov[:, ncmp - 1] = False
    ov4 = jnp.asarray(np.tile(ov.astype(np.float32), (1, Q_PER_KV)), BF16)
    kv_spec = lambda which: pl.BlockSpec((1, HEAD_DIM, L), lambda b, h, i: (b, which * KV_HEADS + h, 0))
    return pl.pallas_call(
        _attn_kernel,
        grid=(B, KV_HEADS, L // _TQ),
        in_specs=[
            pl.BlockSpec((1, Q_PER_KV * HEAD_DIM, _TQ), lambda b, h, i: (b, h, i)),
            kv_spec(0), kv_spec(1), kv_spec(2), kv_spec(3),
            pl.BlockSpec((1, 1, ncmp, HEAD_DIM), lambda b, h, i: (b, h, 0, 0)),
            pl.BlockSpec((1, 1, HEAD_DIM, ncmp), lambda b, h, i: (b, h, 0, 0)),
            pl.BlockSpec((1, _TQ, LANES), lambda b, h, i: (b, i, 0)),
            pl.BlockSpec(ov4.shape, lambda b, h, i: (0, 0)),
        ],
        out_specs=pl.BlockSpec((1, Q_PER_KV * HEAD_DIM, _TQ), lambda b, h, i: (b, h, i)),
        out_shape=jax.ShapeDtypeStruct((B, ATTN_WIDTH, L), F32),
        scratch_shapes=[
            pltpu.VMEM((L, HEAD_DIM), BF16),
            pltpu.VMEM((L, HEAD_DIM), BF16),
            pltpu.VMEM((nslc, _TQ), F32),
            pltpu.VMEM((LANES, _TQ), F32),
        ],
        compiler_params=_params(("parallel", "parallel", "arbitrary")),
        name="nsa_attention",
    )(qt, kvt, kvt, kvt, kvt, kc, vct, dtg, ov4)


_OUTPROJ_TM = 256


def _outproj_kernel(x_ref, ys_ref, ot_ref, anw_ref, wo_ref, g_ref, b_ref, h_ref):
    ot = ot_ref[0]
    ms = jnp.mean(ot * ot, axis=0, keepdims=True)
    y_att = (ot * lax.rsqrt(ms + RMS_EPS) * anw_ref[...]).T.astype(BF16)
    mix = _dot(ys_ref[0], wo_ref[0:SSD_WIDTH, :]) + _dot(y_att, wo_ref[SSD_WIDTH:, :])
    h_ref[0] = _layer_norm_rows(DN_ALPHA * x_ref[0] + mix, g_ref[...], b_ref[...])


def _out_projection(x, y_ssd, o_t, attn_norm_w, w_out, ln_g, ln_b):
    B, L, D = x.shape
    tm = min(_OUTPROJ_TM, L)
    const = lambda shape: pl.BlockSpec(shape, lambda b, i: (0,) * len(shape))
    return pl.pallas_call(
        _outproj_kernel,
        grid=(B, L // tm),
        in_specs=[
            pl.BlockSpec((1, tm, D), lambda b, i: (b, i, 0)),
            pl.BlockSpec((1, tm, SSD_WIDTH), lambda b, i: (b, i, 0)),
            pl.BlockSpec((1, ATTN_WIDTH, tm), lambda b, i: (b, 0, i)),
            const((ATTN_WIDTH, 1)), const((SSD_WIDTH + ATTN_WIDTH, D)), const((1, D)), const((1, D)),
        ],
        out_specs=pl.BlockSpec((1, tm, D), lambda b, i: (b, i, 0)),
        out_shape=jax.ShapeDtypeStruct((B, L, D), F32),
        compiler_params=_params(("parallel", "parallel")),
        name="out_projection",
    )(x, y_ssd, o_t, attn_norm_w.reshape(-1, 1), w_out.astype(BF16), ln_g.reshape(1, -1),
      ln_b.reshape(1, -1))


_PEER_TT = 512
_PEER_IC = 4


def _extract_max(x, idx):
    m = jnp.max(x, axis=0, keepdims=True)
    first = jnp.min(jnp.where(x == m, idx, x.shape[0]), axis=0, keepdims=True)
    return m, jnp.where(idx == first, -jnp.inf, x)


def _peer_route(hb_sc, wq_ref, sk_ref, s_sc, e_sc, thr_sc, top_sc, work_sc):
    tt = hb_sc.shape[0]
    kidx = lax.broadcasted_iota(jnp.int32, (PEER_KEYS, tt), 0)
    cidx = lax.broadcasted_iota(jnp.int32, (PEER_TOPK * PEER_TOPK, tt), 0)

    def head_body(hd, carry):
        for c in range(2):
            col = pl.multiple_of((hd * 2 + c) * PEER_KEYS, PEER_KEYS)
            q = _dot(hb_sc[...], wq_ref[:, pl.ds(col, PEER_KEYS)]).astype(BF16)
            s = _dot_nt(sk_ref[hd, c], q)
            s_sc[hd, c] = s
            work_sc[0:PEER_KEYS, :] = s

            def key_round(r, rc, c=c):
                m, rest = _extract_max(work_sc[0:PEER_KEYS, :], kidx)
                work_sc[0:PEER_KEYS, :] = rest
                top_sc[c, pl.ds(r, 1), :] = m
                return rc

            lax.fori_loop(0, PEER_TOPK, key_round, 0)
        a1 = top_sc[0]
        a2 = top_sc[1]
        for r in range(PEER_TOPK):
            work_sc[r * PEER_TOPK:(r + 1) * PEER_TOPK, :] = a1[r:r + 1, :] + a2
        mx = a1[0:1, :] + a2[0:1, :]

        def pair_round(r, rc):
            zsum, _ = rc
            m, rest = _extract_max(work_sc[...], cidx)
            work_sc[...] = rest
            return zsum + jnp.exp(m - mx), m

        zsum, kth = lax.fori_loop(0, PEER_TOPK, pair_round, (jnp.zeros((1, tt), F32), mx))
        thr_sc[hd] = jnp.broadcast_to(kth, (SUBLANES, tt))
        e_sc[hd, 0] = jnp.exp(s_sc[hd, 0] - a1[0:1, :])
        e_sc[hd, 1] = jnp.exp(s_sc[hd, 1] - a2[0:1, :]) / zsum
        return carry

    lax.fori_loop(0, PEER_HEADS, head_body, 0)


def _peer_kernel(h_ref, wq_ref, sk_ref, u_ref, vt_ref, g_ref, b_ref, o_ref,
                 hb_sc, s_sc, e_sc, thr_sc, top_sc, work_sc, acc_sc):
    ec = pl.program_id(1)
    tt = h_ref.shape[0]

    @pl.when(ec == 0)
    def _():
        hb_sc[...] = h_ref[...].astype(BF16)
        _peer_route(hb_sc, wq_ref, sk_ref, s_sc, e_sc, thr_sc, top_sc, work_sc)
        acc_sc[...] = jnp.zeros(acc_sc.shape, F32)

    hid = _gelu_erf(_dot_nt(u_ref[...], hb_sc[...]))
    gates = []
    for ii in range(_PEER_IC):
        i = ec * _PEER_IC + ii
        g_i = jnp.zeros((PEER_KEYS, tt), F32)
        for hd in range(PEER_HEADS):
            s1 = s_sc[hd, 0, pl.ds(i, 1), :]
            e1 = e_sc[hd, 0, pl.ds(i, 1), :]
            picked = (s1 + s_sc[hd, 1]) >= thr_sc[hd, 0:1, :]
            g_i = g_i + jnp.where(picked, e_sc[hd, 1] * e1, 0.0)
        gates.append(g_i)
    gate = jnp.concatenate(gates, axis=0)
    acc_sc[...] += _dot(vt_ref[...], (gate * hid).astype(BF16))

    @pl.when(ec == pl.num_programs(1) - 1)
    def _():
        ffn = acc_sc[...].T
        o_ref[...] = _layer_norm_rows(DN_ALPHA * h_ref[...] + ffn, g_ref[...], b_ref[...])


def _peer_ffn(h, w_query, sub_keys, expert_u, expert_v, ln_g, ln_b):
    T, D = h.shape
    tt = min(_PEER_TT, T)
    chunk = _PEER_IC * PEER_KEYS
    const = lambda shape: pl.BlockSpec(shape, lambda t, e: (0,) * len(shape))
    return pl.pallas_call(
        _peer_kernel,
        grid=(T // tt, PEER_EXPERTS // chunk),
        in_specs=[
            pl.BlockSpec((tt, D), lambda t, e: (t, 0)),
            const(w_query.shape), const(sub_keys.shape),
            pl.BlockSpec((chunk, D), lambda t, e: (e, 0)),
            pl.BlockSpec((D, chunk), lambda t, e: (0, e)),
            const((1, D)), const((1, D)),
        ],
        out_specs=pl.BlockSpec((tt, D), lambda t, e: (t, 0)),
        out_shape=jax.ShapeDtypeStruct((T, D), F32),
        scratch_shapes=[
            pltpu.VMEM((tt, D), BF16),
            pltpu.VMEM((PEER_HEADS, 2, PEER_KEYS, tt), F32),
            pltpu.VMEM((PEER_HEADS, 2, PEER_KEYS, tt), F32),
            pltpu.VMEM((PEER_HEADS, SUBLANES, tt), F32),
            pltpu.VMEM((2, PEER_TOPK, tt), F32),
            pltpu.VMEM((PEER_TOPK * PEER_TOPK, tt), F32),
            pltpu.VMEM((D, tt), F32),
        ],
        compiler_params=_params(("parallel", "arbitrary")),
        name="peer_ffn",
    )(h, w_query.astype(BF16), sub_keys.astype(BF16), expert_u.astype(BF16),
      expert_v.T.astype(BF16), ln_g.reshape(1, -1), ln_b.reshape(1, -1))


def kernel(x, w_in, conv_w, conv_b, dt_bias, a_log, d_skip, ssd_norm_w, cmp_pos_k, cmp_w1_k, cmp_w2_k,
           cmp_pos_v, cmp_w1_v, cmp_w2_v, attn_norm_w, w_out, ln1_g, ln1_b, peer_w_query, peer_sub_keys,
           peer_u, peer_v, ln2_g, ln2_b):
    B, L, D = x.shape
    h = x
    for i in range(w_in.shape[0]):
        z, xbc, dtg, kvc, qt, kvt = _in_projection(h, w_in[i])
        y_ssd = _ssd_mixer(z, xbc, dtg, conv_w[i], conv_b[i], dt_bias[i], a_log[i], d_skip[i],
                           ssd_norm_w[i])
        kc, vct = _compress(kvc, cmp_pos_k[i], cmp_w1_k[i], cmp_w2_k[i],
                            cmp_pos_v[i], cmp_w1_v[i], cmp_w2_v[i])
        o_t = _nsa_attention(qt, kvt, kc, vct, dtg)
        h1 = _out_projection(h, y_ssd, o_t, attn_norm_w[i], w_out[i], ln1_g[i], ln1_b[i])
        h = _peer_ffn(h1.reshape(B * L, D), peer_w_query[i], peer_sub_keys[i], peer_u[i], peer_v[i],
                      ln2_g[i], ln2_b[i]).reshape(B, L, D)
    return h
```
